```python
import math
import jax, jax.numpy as jnp
from jax import lax
import numpy as np

D_MODEL = 1024
BATCH = 4
SEQ = 8192
DEPTH = 2

M_HEADS = 4
M_HEAD_DIM = 128
M_WIDTH = M_HEADS * M_HEAD_DIM
H_HEADS = 4
H_KEY_DIM = 128
H_VAL_DIM = 128
H_KWIDTH = H_HEADS * H_KEY_DIM
H_VWIDTH = H_HEADS * H_VAL_DIM
R_WIDTH = 512
R_BLOCKS = 4
R_BLOCK_DIM = R_WIDTH // R_BLOCKS
R_GATE_C = 8.0
CONV_WIDTH = 4
CHUNK = 64
N_BRANCHES = 3
D_FF = -(-8 * D_MODEL // (3 * 256)) * 256
DN_ALPHA = (2 * DEPTH) ** 0.25
DN_BETA = (8 * DEPTH) ** -0.25
LN_EPS = 1e-5
NORM_EPS = 1e-6
IN_SPLITS = (M_WIDTH, M_WIDTH, M_WIDTH, M_WIDTH, M_HEADS, M_HEADS,
             H_KWIDTH, H_KWIDTH, H_VWIDTH, H_VWIDTH, R_WIDTH, R_WIDTH,
             N_BRANCHES * D_MODEL)
D_IN = sum(IN_SPLITS)

kernel_name = "hybrid_mlstm_hgrn2_rglru_deepnorm"


def layer_norm(x, g, b):
    xf = x.astype(jnp.float32)
    mu = jnp.mean(xf, -1, keepdims=True)
    var = jnp.mean(jnp.square(xf - mu), -1, keepdims=True)
    return ((xf - mu) * lax.rsqrt(var + LN_EPS) * g + b).astype(x.dtype)


def head_rms_norm(h, g):
    h = h * lax.rsqrt(jnp.mean(jnp.square(h), -1, keepdims=True) + NORM_EPS)
    return h.reshape(h.shape[0], h.shape[1], -1) * g.astype(jnp.float32)


def causal_depthwise_conv(x, w, b):
    y = lax.conv_general_dilated(
        x, w[:, None, :].astype(x.dtype), window_strides=(1,),
        padding=((CONV_WIDTH - 1, 0),), dimension_numbers=('NWC', 'WIO', 'NWC'),
        feature_group_count=x.shape[-1])
    return y + b


def split_heads(t, d):
    b, s = t.shape[:2]
    return t.reshape(b, s, -1, d).transpose(0, 2, 1, 3).astype(jnp.float32)


def to_chunks(t):
    b, h, s = t.shape[:3]
    return jnp.moveaxis(t.reshape(b, h, s // CHUNK, CHUNK, *t.shape[3:]), 2, 0)


def from_chunks(t):
    n, b, h, c = t.shape[:4]
    return jnp.moveaxis(t, 0, 2).reshape(b, h, n * c, *t.shape[4:])


def mlstm_chunkwise(q, k, v, log_i, log_f):
    bsz, nh, _, d = q.shape
    causal = jnp.tril(jnp.ones((CHUNK, CHUNK), bool))

    def step(carry, inp):
        c_st, n_st, m_st = carry
        q_, k_, v_, li, lf = inp
        b = jnp.cumsum(lf, axis=-1)
        log_d = jnp.where(causal, b[..., :, None] - b[..., None, :] + li[..., None, :], -jnp.inf)
        log_inter = b + m_st[..., None]
        m_t = jnp.maximum(jnp.max(log_d, -1), log_inter)
        w_intra = jnp.exp(log_d - m_t[..., None])
        w_inter = jnp.exp(log_inter - m_t)
        scores = jnp.einsum('bhtd,bhsd->bhts', q_, k_) * w_intra
        num = (jnp.einsum('bhts,bhse->bhte', scores, v_)
               + w_inter[..., None] * jnp.einsum('bhtd,bhde->bhte', q_, c_st))
        den = jnp.sum(scores, -1) + w_inter * jnp.einsum('bhtd,bhd->bht', q_, n_st)
        h = num / jnp.maximum(jnp.abs(den), jnp.exp(-m_t))[..., None]
        log_s = b[..., -1:] - b + li
        log_c = b[..., -1] + m_st
        m_new = jnp.maximum(jnp.max(log_s, -1), log_c)
        w_s = jnp.exp(log_s - m_new[..., None])
        w_c = jnp.exp(log_c - m_new)
        c_new = w_c[..., None, None] * c_st + jnp.einsum('bhs,bhsd,bhse->bhde', w_s, k_, v_)
        n_new = w_c[..., None] * n_st + jnp.einsum('bhs,bhsd->bhd', w_s, k_)
        return (c_new, n_new, m_new), h

    init = (jnp.zeros((bsz, nh, d, d), jnp.float32), jnp.zeros((bsz, nh, d), jnp.float32),
            jnp.zeros((bsz, nh), jnp.float32))
    _, hs = lax.scan(step, init, tuple(map(to_chunks, (q, k, v, log_i, log_f))))
    return from_chunks(hs)


def hgrn2_chunkwise(q, k, v, log_f):
    bsz, nh, _, dk = q.shape
    dv = v.shape[-1]
    causal = jnp.tril(jnp.ones((CHUNK, CHUNK), bool))[..., None]

    def step(s_st, inp):
        q_, k_, v_, lf = inp
        b = jnp.cumsum(lf, axis=-2)
        decay = jnp.exp(jnp.where(causal, b[..., :, None, :] - b[..., None, :, :], -jnp.inf))
        a = jnp.einsum('bhtd,bhsd,bhtsd->bhts', q_, k_, decay)
        o = (jnp.einsum('bhts,bhse->bhte', a, v_)
             + jnp.einsum('bhtd,bhde->bhte', q_ * jnp.exp(b), s_st))
        b_last = b[..., -1:, :]
        s_new = (jnp.exp(b_last[..., 0, :])[..., None] * s_st
                 + jnp.einsum('bhsd,bhse->bhde', k_ * jnp.exp(b_last - b), v_))
        return s_new, o

    init = jnp.zeros((bsz, nh, dk, dv), jnp.float32)
    _, os_ = lax.scan(step, init, tuple(map(to_chunks, (q, k, v, log_f))))
    return from_chunks(os_)


def linear_combine(left, right):
    a_l, u_l = left
    a_r, u_r = right
    return a_l * a_r, a_r * u_l + u_r


def rg_lru(u, w_rec, b_rec, w_inp, b_inp, lam):
    bsz, seq, width = u.shape
    uf = u.astype(jnp.float32)
    ub = uf.reshape(bsz, seq, R_BLOCKS, R_BLOCK_DIM)
    r = jax.nn.sigmoid(jnp.einsum('bsni,nij->bsnj', ub, w_rec).reshape(bsz, seq, width) + b_rec)
    i = jax.nn.sigmoid(jnp.einsum('bsni,nij->bsnj', ub, w_inp).reshape(bsz, seq, width) + b_inp)
    log_a = -R_GATE_C * r * jax.nn.softplus(-lam.astype(jnp.float32))
    a = jnp.exp(log_a)
    gated = jnp.sqrt(-jnp.expm1(2.0 * log_a)) * (i * uf)
    _, h = lax.associative_scan(linear_combine, (a, gated), axis=1)
    return h.astype(u.dtype)


def token_mixer(x, w_in, m_conv_w, m_conv_b, m_bias_i, m_bias_f, m_norm_g, lower_bound,
                h_norm_g, r_conv_w, r_conv_b, r_w_rec, r_b_rec, r_w_in, r_b_in, r_lambda,
                w_branch_m, w_branch_h, w_branch_r, w_out):
    f32 = jnp.float32
    bsz, seq, _ = x.shape
    proj = x @ w_in
    idx = np.cumsum(IN_SPLITS)[:-1].tolist()
    mq, mk, mv, mo, mi, mf, hq, hf, hi, hg, rx, rg, gates = jnp.split(proj, idx, axis=-1)

    qk = jax.nn.silu(causal_depthwise_conv(jnp.concatenate([mq, mk], -1), m_conv_w, m_conv_b))
    mq, mk = jnp.split(qk, 2, axis=-1)
    log_i = (mi + m_bias_i).astype(f32).transpose(0, 2, 1)
    log_f = jax.nn.log_sigmoid((mf + m_bias_f).astype(f32)).transpose(0, 2, 1)
    h_m = mlstm_chunkwise(split_heads(mq, M_HEAD_DIM), split_heads(mk, M_HEAD_DIM) * M_HEAD_DIM ** -0.5,
                          split_heads(mv, M_HEAD_DIM), log_i, log_f)
    h_m = h_m.transpose(0, 2, 1, 3) * jax.nn.sigmoid(mo.astype(f32)).reshape(bsz, seq, M_HEADS, M_HEAD_DIM)
    y_m = head_rms_norm(h_m, m_norm_g).astype(x.dtype)

    z = hf.astype(f32)
    lb = lower_bound.astype(f32)
    log_f_h = jnp.logaddexp(jnp.log(lb), jnp.log1p(-lb) + jax.nn.log_sigmoid(z))
    k_h = (1.0 - lb) * jax.nn.sigmoid(-z)
    o_h = hgrn2_chunkwise(split_heads(jax.nn.silu(hq), H_KEY_DIM), split_heads(k_h, H_KEY_DIM),
                          split_heads(hi, H_VAL_DIM), split_heads(log_f_h, H_KEY_DIM))
    y_h = (head_rms_norm(o_h.transpose(0, 2, 1, 3), h_norm_g)
           * jax.nn.sigmoid(hg.astype(f32))).astype(x.dtype)

    u = causal_depthwise_conv(rx, r_conv_w, r_conv_b)
    y_r = rg_lru(u, r_w_rec, r_b_rec, r_w_in, r_b_in, r_lambda) * jax.nn.gelu(rg)

    g_m, g_h, g_r = jnp.split(jax.nn.sigmoid(gates), N_BRANCHES, axis=-1)
    mixed = g_m * (y_m @ w_branch_m) + g_h * (y_h @ w_branch_h) + g_r * (y_r @ w_branch_r)
    return mixed @ w_out


def swiglu(x, w_gate, w_up, w_down):
    return (jax.nn.silu(x @ w_gate) * (x @ w_up)) @ w_down


def setup_inputs(seed: int = 0) -> dict:
    key = jax.random.key(seed)
    ks = jax.random.split(key, 32)
    L = DEPTH

    def nrm(k, shape, scale):
        return jax.random.normal(k, shape, jnp.float32) * scale

    u = jax.random.uniform(ks[14], (L, R_WIDTH), jnp.float32, 0.9, 0.999)
    a0 = u ** (1.0 / R_GATE_C)
    return {
        "x": nrm(ks[0], (BATCH, SEQ, D_MODEL), 1.0),
        "w_in": nrm(ks[1], (L, D_MODEL, D_IN), D_MODEL ** -0.5),
        "m_conv_w": nrm(ks[2], (L, CONV_WIDTH, 2 * M_WIDTH), CONV_WIDTH ** -0.5),
        "m_conv_b": nrm(ks[3], (L, 2 * M_WIDTH), 0.01),
        "m_bias_i": nrm(ks[4], (L, M_HEADS), 0.1),
        "m_bias_f": jnp.linspace(3.0, 6.0, M_HEADS)[None, :] + nrm(ks[5], (L, M_HEADS), 0.1),
        "m_norm_g": 1.0 + nrm(ks[6], (L, M_WIDTH), 0.02),
        "h_lower_bounds": nrm(ks[7], (L, H_KWIDTH), 0.1),
        "h_norm_g": 1.0 + nrm(ks[8], (L, H_VWIDTH), 0.02),
        "r_conv_w": nrm(ks[9], (L, CONV_WIDTH, R_WIDTH), CONV_WIDTH ** -0.5),
        "r_conv_b": nrm(ks[10], (L, R_WIDTH), 0.01),
        "r_w_rec": nrm(ks[11], (L, R_BLOCKS, R_BLOCK_DIM, R_BLOCK_DIM), R_BLOCK_DIM ** -0.5),
        "r_b_rec": nrm(ks[12], (L, R_WIDTH), 0.01),
        "r_w_in": nrm(ks[13], (L, R_BLOCKS, R_BLOCK_DIM, R_BLOCK_DIM), R_BLOCK_DIM ** -0.5),
        "r_b_in": nrm(ks[15], (L, R_WIDTH), 0.01),
        "r_lambda": jnp.log(a0) - jnp.log1p(-a0),
        "w_branch_m": nrm(ks[16], (L, M_WIDTH, D_MODEL), DN_BETA * M_WIDTH ** -0.5),
        "w_branch_h": nrm(ks[17], (L, H_VWIDTH, D_MODEL), DN_BETA * H_VWIDTH ** -0.5),
        "w_branch_r": nrm(ks[18], (L, R_WIDTH, D_MODEL), DN_BETA * R_WIDTH ** -0.5),
        "w_out": nrm(ks[19], (L, D_MODEL, D_MODEL), DN_BETA * D_MODEL ** -0.5),
        "ln1_g": 1.0 + nrm(ks[20], (L, D_MODEL), 0.02),
        "ln1_b": nrm(ks[21], (L, D_MODEL), 0.01),
        "w_ff_gate": nrm(ks[22], (L, D_MODEL, D_FF), DN_BETA * D_MODEL ** -0.5),
        "w_ff_up": nrm(ks[23], (L, D_MODEL, D_FF), DN_BETA * D_MODEL ** -0.5),
        "w_ff_down": nrm(ks[24], (L, D_FF, D_MODEL), DN_BETA * D_FF ** -0.5),
        "ln2_g": 1.0 + nrm(ks[25], (L, D_MODEL), 0.02),
        "ln2_b": nrm(ks[26], (L, D_MODEL), 0.01),
    }


def reference(x, w_in, m_conv_w, m_conv_b, m_bias_i, m_bias_f, m_norm_g, h_lower_bounds,
              h_norm_g, r_conv_w, r_conv_b, r_w_rec, r_b_rec, r_w_in, r_b_in, r_lambda,
              w_branch_m, w_branch_h, w_branch_r, w_out, ln1_g, ln1_b, w_ff_gate, w_ff_up,
              w_ff_down, ln2_g, ln2_b):
    lb_all = jnp.cumsum(jax.nn.softmax(h_lower_bounds.astype(jnp.float32), axis=0), axis=0)
    lb_all = lb_all - lb_all[0]
    for l in range(DEPTH):
        mix = token_mixer(x, w_in[l], m_conv_w[l], m_conv_b[l], m_bias_i[l], m_bias_f[l], m_norm_g[l],
                          lb_all[l], h_norm_g[l], r_conv_w[l], r_conv_b[l], r_w_rec[l], r_b_rec[l],
                          r_w_in[l], r_b_in[l], r_lambda[l], w_branch_m[l], w_branch_h[l],
                          w_branch_r[l], w_out[l])
        x = layer_norm(DN_ALPHA * x + mix, ln1_g[l], ln1_b[l])
        x = layer_norm(DN_ALPHA * x + swiglu(x, w_ff_gate[l], w_ff_up[l], w_ff_down[l]), ln2_g[l], ln2_b[l])
    return x
```

```python
import functools

import jax
import jax.numpy as jnp
from jax import lax
from jax.experimental import pallas as pl
from jax.experimental.pallas import tpu as pltpu

F32 = jnp.float32
MXU_DTYPE = jnp.bfloat16

D_MODEL = 1024
HEADS = 4
HEAD_DIM = 128
WIDTH = HEADS * HEAD_DIM
R_BLOCKS = 4
R_GATE_C = 8.0
CONV_WIDTH = 4
LN_EPS = 1e-5
NORM_EPS = 1e-6

LANES = 128
SUBLANES = 8
VMEM_LIMIT_BYTES = 56 * 1024 * 1024

SEQ_TILE = 256
FFN_TILE = 512
NEG_BIG = -1e30

C_CONV = 0
C_MV = 3 * WIDTH
C_MO = 4 * WIDTH
C_HQ = 5 * WIDTH
C_HF = 6 * WIDTH
C_HI = 7 * WIDTH
C_HG = 8 * WIDTH
C_RG = 9 * WIDTH
C_GATES = 10 * WIDTH
N_MAIN = 10 * WIDTH + 3 * D_MODEL


def _dot(a, b):
    return jnp.dot(a.astype(MXU_DTYPE), b.astype(MXU_DTYPE), preferred_element_type=F32)


def _dot_nt(a, b):
    return lax.dot_general(a.astype(MXU_DTYPE), b.astype(MXU_DTYPE),
                           (((1,), (1,)), ((), ())), preferred_element_type=F32)


def _dot_tn(a, b):
    return lax.dot_general(a.astype(MXU_DTYPE), b.astype(MXU_DTYPE),
                           (((0,), (0,)), ((), ())), preferred_element_type=F32)


def _sigmoid(x):
    return 1.0 / (1.0 + jnp.exp(-x))


def _silu(x):
    return x * _sigmoid(x)


def _softplus(x):
    return jnp.maximum(x, 0.0) + jnp.log1p(jnp.exp(-jnp.abs(x)))


def _log_sigmoid(x):
    return -_softplus(-x)


def _gelu_tanh(x):
    c = 0.7978845608028654
    return 0.5 * x * (1.0 + jnp.tanh(c * (x + 0.044715 * (x * x * x))))


def _layer_norm(r, g, b):
    mu = jnp.mean(r, axis=-1, keepdims=True)
    d = r - mu
    var = jnp.mean(d * d, axis=-1, keepdims=True)
    return d * lax.rsqrt(var + LN_EPS) * g + b


def _head_rms(h):
    return h * lax.rsqrt(jnp.mean(h * h, axis=-1, keepdims=True) + NORM_EPS)


def _shift_rows(x, s, row, fill):
    return jnp.where(row >= s, pltpu.roll(x, s, 0), fill)


def _cumsum_rows(x, row):
    s = 1
    while s < x.shape[0]:
        x = x + _shift_rows(x, s, row, 0.0)
        s *= 2
    return x


def _group_ref_rows(b, blk):
    n, w = b.shape
    if blk >= SUBLANES:
        grp = 2 * blk
        parts = [jnp.broadcast_to(b[g * grp + blk - 1:g * grp + blk, :], (grp, w))
                 for g in range(n // grp)]
        return jnp.concatenate(parts, axis=0) if len(parts) > 1 else parts[0]
    sub = lax.broadcasted_iota(jnp.int32, (SUBLANES, w), 0)
    parts = []
    for g in range(n // SUBLANES):
        base = g * SUBLANES
        rows = [jnp.broadcast_to(b[base + r:base + r + 1, :], (SUBLANES, w))
                for r in range(blk - 1, SUBLANES, 2 * blk)]
        e = rows[-1]
        for i in range(len(rows) - 2, -1, -1):
            e = jnp.where(sub < (i + 1) * 2 * blk, rows[i], e)
        parts.append(e)
    return jnp.concatenate(parts, axis=0)


def _mixer_kernel(x_ref, wmain_ref, wg_ref, convw_ref, convb_ref, gbias_ref, mnorm_ref,
                  hnorm_ref, lb_ref, wr_ref, rb_ref, rlam_ref, wbr_ref, wout_ref,
                  lng_ref, lnb_ref, o_ref,
                  conv_buf, m_state, m_max, h_state, r_state, *, alpha):
    tt = x_ref.shape[1]
    j = pl.program_id(1)

    @pl.when(j == 0)
    def _reset():
        conv_buf[0:SUBLANES, :] = jnp.zeros((SUBLANES, conv_buf.shape[1]), F32)
        m_state[...] = jnp.zeros(m_state.shape, F32)
        m_max[...] = jnp.zeros(m_max.shape, F32)
        h_state[...] = jnp.zeros(h_state.shape, F32)
        r_state[...] = jnp.zeros(r_state.shape, F32)

    x = x_ref[0]
    xb = x.astype(MXU_DTYPE)

    row_w = lax.broadcasted_iota(jnp.int32, (tt, WIDTH), 0)
    t_idx = lax.broadcasted_iota(jnp.int32, (tt, tt), 0)
    s_idx = lax.broadcasted_iota(jnp.int32, (tt, tt), 1)
    causal = t_idx >= s_idx
    strictly_lower = t_idx > s_idx
    t_xor_s = jnp.bitwise_xor(t_idx, s_idx)

    conv_buf[SUBLANES:SUBLANES + tt, :] = _dot(xb, wmain_ref[:, C_CONV:C_MV])
    conv = convb_ref[...]
    for jj in range(CONV_WIDTH):
        shifted = conv_buf[pl.ds(SUBLANES - (CONV_WIDTH - 1) + jj, tt), :]
        conv = conv + convw_ref[jj:jj + 1, :] * shifted
    conv_buf[0:SUBLANES, :] = conv_buf[tt:tt + SUBLANES, :]
    qk = _silu(conv[:, :2 * WIDTH])
    u = conv[:, 2 * WIDTH:]

    gpre = _dot(xb, wg_ref[...]) + gbias_ref[...]
    logf = _log_sigmoid(gpre)
    bcum = jnp.dot(causal.astype(F32), logf, precision=lax.Precision.HIGHEST,
                   preferred_element_type=F32)
    lane_g = lax.broadcasted_iota(jnp.int32, (tt, LANES), 1)
    zc = jnp.where(lane_g < HEADS, gpre, bcum)
    zr = zc.T

    mv = _dot(xb, wmain_ref[:, C_MV:C_MO])
    mo = _dot(xb, wmain_ref[:, C_MO:C_HQ])
    ones_v = jnp.ones((tt, HEAD_DIM), F32)
    ym_parts = []
    for h in range(HEADS):
        sl = slice(h * HEAD_DIM, (h + 1) * HEAD_DIM)
        q = qk[:, sl]
        k = qk[:, WIDTH + h * HEAD_DIM:WIDTH + (h + 1) * HEAD_DIM] * (HEAD_DIM ** -0.5)
        v_aug = jnp.concatenate([mv[:, sl], ones_v], axis=1)
        li_row = zr[h:h + 1, :]
        b_row = zr[HEADS + h:HEADS + h + 1, :]
        li_col = zc[:, h:h + 1]
        b_col = zc[:, HEADS + h:HEADS + h + 1]
        m_prev = m_max[h:h + 1, 0:1]
        c_aug = m_state[h]

        log_d = jnp.where(causal, b_col - b_row + li_row, NEG_BIG)
        log_inter = b_col + m_prev
        m_t = jnp.maximum(jnp.max(log_d, axis=1, keepdims=True), log_inter)
        w_intra = jnp.exp(log_d - m_t)
        w_inter = jnp.exp(log_inter - m_t)
        scores = _dot_nt(q, k) * w_intra
        numden = _dot(scores, v_aug) + w_inter * _dot(q, c_aug)
        num = numden[:, :HEAD_DIM]
        den = numden[:, HEAD_DIM:]
        hm = num / jnp.maximum(jnp.abs(den), jnp.exp(-m_t))

        b_last = b_col[tt - 1:tt, :]
        log_s = b_last - b_col + li_col
        log_c = b_last + m_prev
        m_new = jnp.maximum(jnp.max(log_s, axis=0, keepdims=True), log_c)
        w_s = jnp.exp(log_s - m_new)
        w_c = jnp.exp(log_c - m_new)
        m_state[h] = w_c * c_aug + _dot_tn(k * w_s, v_aug)
        m_max[h:h + 1, :] = jnp.broadcast_to(m_new, (1, LANES))

        hm = hm * _sigmoid(mo[:, sl])
        ym_parts.append(_head_rms(hm) * mnorm_ref[:, sl])
    y_m = jnp.concatenate(ym_parts, axis=1)
    mix = _sigmoid(_dot(xb, wmain_ref[:, C_GATES:C_GATES + D_MODEL])) * _dot(y_m, wbr_ref[0])

    hq = _silu(_dot(xb, wmain_ref[:, C_HQ:C_HF]))
    z = _dot(xb, wmain_ref[:, C_HF:C_HI])
    hi = _dot(xb, wmain_ref[:, C_HI:C_HG])
    lb = lb_ref[...]
    la = jnp.log(lb)
    lc = jnp.log1p(-lb) + _log_sigmoid(z)
    lf = jnp.maximum(la, lc) + jnp.log1p(jnp.exp(-jnp.abs(la - lc)))
    hk = (1.0 - lb) * _sigmoid(-z)
    b = _cumsum_rows(lf, row_w)

    a_h = [jnp.where(t_idx == s_idx, _dot_nt(hq[:, h * HEAD_DIM:(h + 1) * HEAD_DIM],
                                             hk[:, h * HEAD_DIM:(h + 1) * HEAD_DIM]), 0.0)
           for h in range(HEADS)]
    blk = 1
    shift = 0
    while blk < tt:
        w = jnp.exp(-jnp.abs(b - _group_ref_rows(b, blk)))
        qw = (hq * w).astype(MXU_DTYPE)
        kw = (hk * w).astype(MXU_DTYPE)
        level = jnp.logical_and(jnp.right_shift(t_xor_s, shift) == 1, strictly_lower)
        for h in range(HEADS):
            sl = slice(h * HEAD_DIM, (h + 1) * HEAD_DIM)
            a_h[h] = a_h[h] + jnp.where(level, _dot_nt(qw[:, sl], kw[:, sl]), 0.0)
        blk *= 2
        shift += 1

    b_last = b[tt - 1:tt, :]
    q_dec = hq * jnp.exp(b)
    k_dec = hk * jnp.exp(b_last - b)
    s_dec = jnp.exp(b_last)
    yh_parts = []
    for h in range(HEADS):
        sl = slice(h * HEAD_DIM, (h + 1) * HEAD_DIM)
        st = h_state[h]
        o = _dot(a_h[h], hi[:, sl]) + _dot_nt(q_dec[:, sl], st)
        h_state[h] = st * s_dec[:, sl] + _dot_tn(hi[:, sl], k_dec[:, sl])
        yh_parts.append(_head_rms(o) * hnorm_ref[:, sl])
    y_h = jnp.concatenate(yh_parts, axis=1) * _sigmoid(_dot(xb, wmain_ref[:, C_HG:C_RG]))
    mix = mix + (_sigmoid(_dot(xb, wmain_ref[:, C_GATES + D_MODEL:C_GATES + 2 * D_MODEL]))
                 * _dot(y_h, wbr_ref[1]))

    ri_parts = [_dot(u[:, n * HEAD_DIM:(n + 1) * HEAD_DIM], wr_ref[n]) for n in range(R_BLOCKS)]
    r_gate = _sigmoid(jnp.concatenate([p[:, :HEAD_DIM] for p in ri_parts], axis=1) + rb_ref[0:1, :])
    i_gate = _sigmoid(jnp.concatenate([p[:, HEAD_DIM:] for p in ri_parts], axis=1) + rb_ref[1:2, :])
    log_a = (-R_GATE_C) * r_gate * _softplus(-rlam_ref[...])
    a = jnp.exp(log_a)
    g = jnp.sqrt(1.0 - a * a) * (i_gate * u)
    s = 1
    while s < tt:
        g = a * _shift_rows(g, s, row_w, 0.0) + g
        a = a * _shift_rows(a, s, row_w, 1.0)
        s *= 2
    h_r = g + a * r_state[0:1, :]
    r_state[...] = jnp.broadcast_to(h_r[tt - 1:tt, :], r_state.shape)
    y_r = h_r * _gelu_tanh(_dot(xb, wmain_ref[:, C_RG:C_GATES]))
    mix = mix + (_sigmoid(_dot(xb, wmain_ref[:, C_GATES + 2 * D_MODEL:N_MAIN]))
                 * _dot(y_r, wbr_ref[2]))

    res = alpha * x + _dot(mix, wout_ref[...])
    o_ref[0] = _layer_norm(res, lng_ref[...], lnb_ref[...])


def _ffn_kernel(x_ref, wgu_ref, wdown_ref, lng_ref, lnb_ref, o_ref, *, alpha):
    x = x_ref[...]
    xb = x.astype(MXU_DTYPE)
    d_ff = wdown_ref.shape[0]
    gate = _dot(xb, wgu_ref[:, :d_ff])
    up = _dot(xb, wgu_ref[:, d_ff:])
    res = alpha * x + _dot(_silu(gate) * up, wdown_ref[...])
    o_ref[...] = _layer_norm(res, lng_ref[...], lnb_ref[...])


def _resident(shape):
    nd = len(shape)
    return pl.BlockSpec(shape, lambda *_: (0,) * nd, pipeline_mode=pl.Buffered(1))


def _mixer_call(x, p, alpha):
    bsz, seq, d = x.shape
    tt = SEQ_TILE
    assert seq % tt == 0 and d == D_MODEL
    operands = [p["w_main"], p["w_gate"], p["conv_w"], p["conv_b"], p["gate_bias"], p["m_norm_g"],
                p["h_norm_g"], p["lb"], p["w_r"], p["r_b"], p["r_lambda"], p["w_branch"],
                p["w_out"], p["ln1_g"], p["ln1_b"]]
    in_specs = [pl.BlockSpec((1, tt, d), lambda b, j: (b, j, 0))]
    in_specs += [_resident(a.shape) for a in operands]
    return pl.pallas_call(
        functools.partial(_mixer_kernel, alpha=alpha),
        grid=(bsz, seq // tt),
        in_specs=in_specs,
        out_specs=pl.BlockSpec((1, tt, d), lambda b, j: (b, j, 0)),
        out_shape=jax.ShapeDtypeStruct(x.shape, F32),
        scratch_shapes=[
            pltpu.VMEM((tt + SUBLANES, 3 * WIDTH), F32),
            pltpu.VMEM((HEADS, HEAD_DIM, 2 * HEAD_DIM), F32),
            pltpu.VMEM((SUBLANES, LANES), F32),
            pltpu.VMEM((HEADS, HEAD_DIM, HEAD_DIM), F32),
            pltpu.VMEM((SUBLANES, WIDTH), F32),
        ],
        compiler_params=pltpu.CompilerParams(
            dimension_semantics=("arbitrary", "arbitrary"),
            vmem_limit_bytes=VMEM_LIMIT_BYTES),
        name="mixer",
    )(x, *operands)


def _ffn_call(x2d, p, alpha):
    n_tok, d = x2d.shape
    tm = FFN_TILE
    assert n_tok % tm == 0
    operands = [p["w_gu"], p["w_down"], p["ln2_g"], p["ln2_b"]]
    in_specs = [pl.BlockSpec((tm, d), lambda i: (i, 0))]
    in_specs += [_resident(a.shape) for a in operands]
    return pl.pallas_call(
        functools.partial(_ffn_kernel, alpha=alpha),
        grid=(n_tok // tm,),
        in_specs=in_specs,
        out_specs=pl.BlockSpec((tm, d), lambda i: (i, 0)),
        out_shape=jax.ShapeDtypeStruct(x2d.shape, F32),
        compiler_params=pltpu.CompilerParams(
            dimension_semantics=("arbitrary",),
            vmem_limit_bytes=VMEM_LIMIT_BYTES),
        name="ffn",
    )(x2d, *operands)


def _prep_layer(l, w_in, m_conv_w, m_conv_b, m_bias_i, m_bias_f, m_norm_g, lb_all, h_norm_g,
                r_conv_w, r_conv_b, r_w_rec, r_b_rec, r_w_in, r_b_in, r_lambda,
                w_branch_m, w_branch_h, w_branch_r, w_out, ln1_g, ln1_b, w_ff_gate, w_ff_up,
                w_ff_down, ln2_g, ln2_b):
    w = w_in[l]
    o = 0
    cols = {}
    for name, width in (("mq", WIDTH), ("mk", WIDTH), ("mv", WIDTH), ("mo", WIDTH),
                        ("mi", HEADS), ("mf", HEADS), ("hq", WIDTH), ("hf", WIDTH),
                        ("hi", WIDTH), ("hg", WIDTH), ("rx", WIDTH), ("rg", WIDTH),
                        ("gates", 3 * D_MODEL)):
        cols[name] = w[:, o:o + width]
        o += width
    w_main = jnp.concatenate([cols[n] for n in ("mq", "mk", "rx", "mv", "mo", "hq", "hf", "hi",
                                                "hg", "rg", "gates")], axis=1).astype(MXU_DTYPE)
    w_gate = jnp.concatenate([cols["mi"], cols["mf"],
                              jnp.zeros((D_MODEL, LANES - 2 * HEADS), F32)], axis=1).astype(MXU_DTYPE)
    gate_bias = jnp.concatenate([m_bias_i[l], m_bias_f[l],
                                 jnp.zeros((LANES - 2 * HEADS,), F32)])[None, :]
    w_r = jnp.concatenate([r_w_rec[l], r_w_in[l]], axis=2).astype(MXU_DTYPE)
    return dict(
        w_main=w_main, w_gate=w_gate,
        conv_w=jnp.concatenate([m_conv_w[l], r_conv_w[l]], axis=1),
        conv_b=jnp.concatenate([m_conv_b[l], r_conv_b[l]])[None, :],
        gate_bias=gate_bias,
        m_norm_g=m_norm_g[l][None, :], h_norm_g=h_norm_g[l][None, :], lb=lb_all[l][None, :],
        w_r=w_r, r_b=jnp.stack([r_b_rec[l], r_b_in[l]]), r_lambda=r_lambda[l][None, :],
        w_branch=jnp.stack([w_branch_m[l], w_branch_h[l], w_branch_r[l]]).astype(MXU_DTYPE),
        w_out=w_out[l].astype(MXU_DTYPE),
        ln1_g=ln1_g[l][None, :], ln1_b=ln1_b[l][None, :],
        w_gu=jnp.concatenate([w_ff_gate[l], w_ff_up[l]], axis=1).astype(MXU_DTYPE),
        w_down=w_ff_down[l].astype(MXU_DTYPE),
        ln2_g=ln2_g[l][None, :], ln2_b=ln2_b[l][None, :],
    )


def kernel(x, w_in, m_conv_w, m_conv_b, m_bias_i, m_bias_f, m_norm_g, h_lower_bounds, h_norm_g, r_conv_w, r_conv_b, r_w_rec, r_b_rec, r_w_in, r_b_in, r_lambda, w_branch_m, w_branch_h, w_branch_r, w_out, ln1_g, ln1_b, w_ff_gate, w_ff_up, w_ff_down, ln2_g, ln2_b):
    depth = w_in.shape[0]
    alpha = float((2 * depth) ** 0.25)
    lb_all = jnp.cumsum(jax.nn.softmax(h_lower_bounds.astype(F32), axis=0), axis=0)
    lb_all = lb_all - lb_all[0]
    bsz, seq, d = x.shape
    for l in range(depth):
        p = _prep_layer(l, w_in, m_conv_w, m_conv_b, m_bias_i, m_bias_f, m_norm_g, lb_all,
                        h_norm_g, r_conv_w, r_conv_b, r_w_rec, r_b_rec, r_w_in, r_b_in, r_lambda,
                        w_branch_m, w_branch_h, w_branch_r, w_out, ln1_g, ln1_b, w_ff_gate,
                        w_ff_up, w_ff_down, ln2_g, ln2_b)
        x = _mixer_call(x, p, alpha)
        x = _ffn_call(x.reshape(bsz * seq, d), p, alpha).reshape(bsz, seq, d)
    return x
```

```python
import functools

import jax
import jax.numpy as jnp
from jax import lax
from jax.experimental import pallas as pl
from jax.experimental.pallas import tpu as pltpu

F32 = jnp.float32
MXU_DTYPE = jnp.bfloat16

D_MODEL = 1024
HEADS = 4
HEAD_DIM = 128
WIDTH = HEADS * HEAD_DIM
R_BLOCKS = 4
R_GATE_C = 8.0
CONV_WIDTH = 4
LN_EPS = 1e-5
NORM_EPS = 1e-6

LANES = 128
SUBLANES = 8
VMEM_LIMIT_BYTES = 56 * 1024 * 1024

SEQ_TILE = 256
FFN_TILE = 512
NEG_BIG = -1e30
LOG2_E = 1.4426950408889634

C_CONV = 0
C_MV = 3 * WIDTH
C_MO = 4 * WIDTH
C_HQ = 5 * WIDTH
C_HF = 6 * WIDTH
C_HI = 7 * WIDTH
C_HG = 8 * WIDTH
C_RG = 9 * WIDTH
C_GATES = 10 * WIDTH
C_MGATE = 10 * WIDTH + 3 * D_MODEL
N_MAIN = C_MGATE + LANES
PAD_COLS = LANES
_PROJ_ORDER = ((C_MGATE, N_MAIN), (C_MV, C_HQ), (C_HQ, C_HG), (C_GATES, C_GATES + D_MODEL),
               (C_HG, C_GATES), (C_GATES + D_MODEL, C_MGATE))


def _dot(a, b):
    return jnp.dot(a.astype(MXU_DTYPE), b.astype(MXU_DTYPE), preferred_element_type=F32)


def _dot_nt(a, b):
    return lax.dot_general(a.astype(MXU_DTYPE), b.astype(MXU_DTYPE),
                           (((1,), (1,)), ((), ())), preferred_element_type=F32)


def _dot_tn(a, b):
    return lax.dot_general(a.astype(MXU_DTYPE), b.astype(MXU_DTYPE),
                           (((0,), (0,)), ((), ())), preferred_element_type=F32)


def _sigmoid(x):
    return 0.5 * jnp.tanh(0.5 * x) + 0.5


def _silu(x):
    hx = 0.5 * x
    return hx * jnp.tanh(hx) + hx


def _softplus(x):
    return jnp.maximum(x, 0.0) + jnp.log(1.0 + jnp.exp(-jnp.abs(x)))


def _log_sigmoid(x):
    return -_softplus(-x)


def _gelu_tanh(x):
    c = 0.7978845608028654
    return 0.5 * x * (1.0 + jnp.tanh(c * (x + 0.044715 * (x * x * x))))


def _layer_norm(r, g, b):
    mu = jnp.mean(r, axis=-1, keepdims=True)
    d = r - mu
    var = jnp.mean(d * d, axis=-1, keepdims=True)
    return d * lax.rsqrt(var + LN_EPS) * g + b


def _head_rms(h):
    return h * lax.rsqrt(jnp.mean(h * h, axis=-1, keepdims=True) + NORM_EPS)


def _scan_rows(g, a, carry, row8):
    n, w = g.shape
    for s in (1, 2, 4):
        keep = row8 >= s
        if a is None:
            g = g + jnp.where(keep, pltpu.roll(g, s, 0), 0.0)
        else:
            g = g + a * jnp.where(keep, pltpu.roll(g, s, 0), 0.0)
            a = a * jnp.where(keep, pltpu.roll(a, s, 0), 1.0)
    outs = []
    c = carry
    for r in range(n // SUBLANES):
        rows = slice(r * SUBLANES, (r + 1) * SUBLANES)
        blk = g[rows]
        if c is not None:
            blk = blk + (c if a is None else a[rows] * c)
        outs.append(blk)
        c = jnp.broadcast_to(blk[SUBLANES - 1:SUBLANES, :], (SUBLANES, w))
    return jnp.concatenate(outs, axis=0)


def _group_ref_rows(b, blk):
    n, w = b.shape
    if blk >= SUBLANES:
        grp = 2 * blk
        parts = [jnp.broadcast_to(b[g * grp + blk - 1:g * grp + blk, :], (grp, w))
                 for g in range(n // grp)]
        return jnp.concatenate(parts, axis=0) if len(parts) > 1 else parts[0]
    sub = lax.broadcasted_iota(jnp.int32, (SUBLANES, w), 0)
    parts = []
    for g in range(n // SUBLANES):
        base = g * SUBLANES
        rows = [jnp.broadcast_to(b[base + r:base + r + 1, :], (SUBLANES, w))
                for r in range(blk - 1, SUBLANES, 2 * blk)]
        e = rows[-1]
        for i in range(len(rows) - 2, -1, -1):
            e = jnp.where(sub < (i + 1) * 2 * blk, rows[i], e)
        parts.append(e)
    return jnp.concatenate(parts, axis=0)


def _mixer_kernel(x_ref, wmain_ref, convw_ref, convb_ref, gbias_ref, mnorm_ref,
                  hnorm_ref, lb_ref, wr_ref, rb_ref, rlam_ref, wbr_ref, wout_ref,
                  lng_ref, lnb_ref, o_ref,
                  conv_buf, proj, m_state, m_max, h_state, r_state, *, alpha):
    tt = x_ref.shape[1]
    j = pl.program_id(1)

    @pl.when(j == 0)
    def _reset():
        conv_buf[0:SUBLANES, :] = jnp.zeros((SUBLANES, conv_buf.shape[1]), F32)
        m_state[...] = jnp.zeros(m_state.shape, F32)
        m_max[...] = jnp.zeros(m_max.shape, F32)
        h_state[...] = jnp.zeros(h_state.shape, F32)
        r_state[...] = jnp.zeros(r_state.shape, F32)

    x = x_ref[0]
    xb = x.astype(MXU_DTYPE)

    row8_w = jnp.bitwise_and(lax.broadcasted_iota(jnp.int32, (tt, WIDTH), 0), SUBLANES - 1)
    row8_g = jnp.bitwise_and(lax.broadcasted_iota(jnp.int32, (tt, LANES), 0), SUBLANES - 1)
    t_idx = lax.broadcasted_iota(jnp.int32, (tt, tt), 0)
    s_idx = lax.broadcasted_iota(jnp.int32, (tt, tt), 1)
    causal = t_idx >= s_idx
    pair_level = jnp.where(t_idx > s_idx, 31 - lax.clz(jnp.bitwise_xor(t_idx, s_idx)),
                           jnp.where(t_idx == s_idx, -1, -2))

    conv_buf[SUBLANES:SUBLANES + tt, :] = _dot(xb, wmain_ref[:, C_CONV:C_MV])
    for lo, hi in _PROJ_ORDER:
        proj[:, lo - C_MV:hi - C_MV] = _dot(xb, wmain_ref[:, lo:hi])

    def pcols(lo, hi):
        return proj[:, lo - C_MV:hi - C_MV]

    conv = convb_ref[...]
    for jj in range(CONV_WIDTH):
        shifted = conv_buf[pl.ds(SUBLANES - (CONV_WIDTH - 1) + jj, tt), :]
        conv = conv + convw_ref[jj:jj + 1, :] * shifted
    conv_buf[0:SUBLANES, :] = conv_buf[tt:tt + SUBLANES, :]
    qk = _silu(conv[:, :2 * WIDTH])
    u = conv[:, 2 * WIDTH:]

    gpre = pcols(C_MGATE, N_MAIN) + gbias_ref[...]
    bcum = _scan_rows(_log_sigmoid(gpre), None, None, row8_g)
    lane_g = lax.broadcasted_iota(jnp.int32, (tt, LANES), 1)
    zc = jnp.where(lane_g < HEADS, gpre, bcum)
    zr = zc.T

    mv = pcols(C_MV, C_MO)
    mo = pcols(C_MO, C_HQ)
    ones_v = jnp.ones((tt, HEAD_DIM), F32)
    ym_parts = []
    for h in range(HEADS):
        sl = slice(h * HEAD_DIM, (h + 1) * HEAD_DIM)
        q = qk[:, sl]
        k = qk[:, WIDTH + h * HEAD_DIM:WIDTH + (h + 1) * HEAD_DIM] * (HEAD_DIM ** -0.5)
        v_aug = jnp.concatenate([mv[:, sl], ones_v], axis=1)
        li_row = zr[h:h + 1, :]
        b_row = zr[HEADS + h:HEADS + h + 1, :]
        li_col = zc[:, h:h + 1]
        b_col = zc[:, HEADS + h:HEADS + h + 1]
        m_prev = m_max[h:h + 1, 0:1]
        c_aug = m_state[h]

        log_d = jnp.where(causal, b_col - b_row + li_row, NEG_BIG)
        log_inter = b_col + m_prev
        m_t = jnp.maximum(jnp.max(log_d, axis=1, keepdims=True), log_inter)
        w_intra = jnp.exp(log_d - m_t)
        w_inter = jnp.exp(log_inter - m_t)
        scores = _dot_nt(q, k) * w_intra
        numden = _dot(scores, v_aug) + w_inter * _dot(q, c_aug)
        num = numden[:, :HEAD_DIM]
        den = numden[:, HEAD_DIM:]
        hm = num / jnp.maximum(jnp.abs(den), jnp.exp(-m_t))

        b_last = b_col[tt - 1:tt, :]
        log_s = b_last - b_col + li_col
        log_c = b_last + m_prev
        m_new = jnp.maximum(jnp.max(log_s, axis=0, keepdims=True), log_c)
        w_s = jnp.exp(log_s - m_new)
        w_c = jnp.exp(log_c - m_new)
        m_state[h] = w_c * c_aug + _dot_tn(k * w_s, v_aug)
        m_max[h:h + 1, :] = jnp.broadcast_to(m_new, (1, LANES))

        hm = hm * _sigmoid(mo[:, sl])
        ym_parts.append(_head_rms(hm) * mnorm_ref[:, sl])
    y_m = jnp.concatenate(ym_parts, axis=1)
    mix = (_sigmoid(pcols(C_GATES, C_GATES + D_MODEL))
           * _dot(y_m, wbr_ref[:, 0:D_MODEL]))

    hq = _silu(pcols(C_HQ, C_HF))
    z = pcols(C_HF, C_HI)
    hi = pcols(C_HI, C_HG)
    lb = lb_ref[...]
    sig_z = _sigmoid(z)
    la = jnp.log(lb)
    lc = jnp.log(1.0 - lb) + _log_sigmoid(z)
    lf = jnp.maximum(la, lc) + jnp.log(1.0 + jnp.exp(-jnp.abs(la - lc)))
    hk = (1.0 - lb) * (1.0 - sig_z)
    b = _scan_rows(lf, None, None, row8_w)
    b2 = b * LOG2_E

    hq_m = hq.astype(MXU_DTYPE)
    hk_m = hk.astype(MXU_DTYPE)
    a_h = [jnp.where(pair_level == -1,
                     _dot_nt(hq_m[:, h * HEAD_DIM:(h + 1) * HEAD_DIM],
                             hk_m[:, h * HEAD_DIM:(h + 1) * HEAD_DIM]), 0.0)
           for h in range(HEADS)]
    blk = 1
    level = 0
    while blk < tt:
        w = jnp.exp2(-jnp.abs(b2 - _group_ref_rows(b2, blk))).astype(MXU_DTYPE)
        qw = hq_m * w
        kw = hk_m * w
        on_level = pair_level == level
        for h in range(HEADS):
            sl = slice(h * HEAD_DIM, (h + 1) * HEAD_DIM)
            a_h[h] = jnp.where(on_level, _dot_nt(qw[:, sl], kw[:, sl]), a_h[h])
        blk *= 2
        level += 1

    b2_last = b2[tt - 1:tt, :]
    q_dec = hq * jnp.exp2(b2)
    k_dec = hk * jnp.exp2(b2_last - b2)
    s_dec = jnp.exp2(b2_last)
    yh_parts = []
    for h in range(HEADS):
        sl = slice(h * HEAD_DIM, (h + 1) * HEAD_DIM)
        st = h_state[h]
        o = _dot(a_h[h], hi[:, sl]) + _dot_nt(q_dec[:, sl], st)
        h_state[h] = st * s_dec[:, sl] + _dot_tn(hi[:, sl], k_dec[:, sl])
        yh_parts.append(_head_rms(o) * hnorm_ref[:, sl])
    y_h = jnp.concatenate(yh_parts, axis=1) * _sigmoid(pcols(C_HG, C_RG))
    mix = mix + (_sigmoid(pcols(C_GATES + D_MODEL, C_GATES + 2 * D_MODEL))
                 * _dot(y_h, wbr_ref[:, D_MODEL:2 * D_MODEL]))

    ri_parts = [_dot(u[:, n * HEAD_DIM:(n + 1) * HEAD_DIM],
                     wr_ref[:, n * 2 * HEAD_DIM:(n + 1) * 2 * HEAD_DIM]) for n in range(R_BLOCKS)]
    r_gate = _sigmoid(jnp.concatenate([p[:, :HEAD_DIM] for p in ri_parts], axis=1) + rb_ref[0:1, :])
    i_gate = _sigmoid(jnp.concatenate([p[:, HEAD_DIM:] for p in ri_parts], axis=1) + rb_ref[1:2, :])
    log_a = (-R_GATE_C) * r_gate * _softplus(-rlam_ref[...])
    a = jnp.exp(log_a)
    g = jnp.sqrt(1.0 - a * a) * (i_gate * u)
    h_r = _scan_rows(g, a, r_state[...], row8_w)
    r_state[...] = jnp.broadcast_to(h_r[tt - 1:tt, :], r_state.shape)
    y_r = h_r * _gelu_tanh(pcols(C_RG, C_GATES))
    mix = mix + (_sigmoid(pcols(C_GATES + 2 * D_MODEL, C_MGATE))
                 * _dot(y_r, wbr_ref[:, 2 * D_MODEL:3 * D_MODEL]))

    res = alpha * x + _dot(mix, wout_ref[:, 0:D_MODEL])
    o_ref[0] = _layer_norm(res, lng_ref[...], lnb_ref[...])


def _ffn_kernel(x_ref, wgu_ref, wdown_ref, lng_ref, lnb_ref, o_ref, *, alpha):
    x = x_ref[...]
    xb = x.astype(MXU_DTYPE)
    d_ff = wdown_ref.shape[0]
    gate = _dot(xb, wgu_ref[:, :d_ff])
    up = _dot(xb, wgu_ref[:, d_ff:])
    res = alpha * x + _dot(_silu(gate) * up, wdown_ref[:, 0:D_MODEL])
    o_ref[...] = _layer_norm(res, lng_ref[...], lnb_ref[...])


def _resident(shape):
    nd = len(shape)
    return pl.BlockSpec(shape, lambda *_: (0,) * nd, pipeline_mode=pl.Buffered(1))


def _mixer_call(x, p, alpha):
    bsz, seq, d = x.shape
    tt = SEQ_TILE
    assert seq % tt == 0 and d == D_MODEL
    operands = [p["w_main"], p["conv_w"], p["conv_b"], p["gate_bias"], p["m_norm_g"],
                p["h_norm_g"], p["lb"], p["w_r"], p["r_b"], p["r_lambda"], p["w_branch"],
                p["w_out"], p["ln1_g"], p["ln1_b"]]
    in_specs = [pl.BlockSpec((1, tt, d), lambda b, j: (b, j, 0))]
    in_specs += [_resident(a.shape) for a in operands]
    return pl.pallas_call(
        functools.partial(_mixer_kernel, alpha=alpha),
        grid=(bsz, seq // tt),
        in_specs=in_specs,
        out_specs=pl.BlockSpec((1, tt, d), lambda b, j: (b, j, 0)),
        out_shape=jax.ShapeDtypeStruct(x.shape, F32),
        scratch_shapes=[
            pltpu.VMEM((tt + SUBLANES, 3 * WIDTH), F32),
            pltpu.VMEM((tt, N_MAIN - C_MV), F32),
            pltpu.VMEM((HEADS, HEAD_DIM, 2 * HEAD_DIM), F32),
            pltpu.VMEM((SUBLANES, LANES), F32),
            pltpu.VMEM((HEADS, HEAD_DIM, HEAD_DIM), F32),
            pltpu.VMEM((SUBLANES, WIDTH), F32),
        ],
        compiler_params=pltpu.CompilerParams(
            dimension_semantics=("arbitrary", "arbitrary"),
            vmem_limit_bytes=VMEM_LIMIT_BYTES),
        name="mixer",
    )(x, *operands)


def _ffn_call(x2d, p, alpha):
    n_tok, d = x2d.shape
    tm = FFN_TILE
    assert n_tok % tm == 0
    operands = [p["w_gu"], p["w_down"], p["ln2_g"], p["ln2_b"]]
    in_specs = [pl.BlockSpec((tm, d), lambda i: (i, 0))]
    in_specs += [_resident(a.shape) for a in operands]
    return pl.pallas_call(
        functools.partial(_ffn_kernel, alpha=alpha),
        grid=(n_tok // tm,),
        in_specs=in_specs,
        out_specs=pl.BlockSpec((tm, d), lambda i: (i, 0)),
        out_shape=jax.ShapeDtypeStruct(x2d.shape, F32),
        compiler_params=pltpu.CompilerParams(
            dimension_semantics=("arbitrary",),
            vmem_limit_bytes=VMEM_LIMIT_BYTES),
        name="ffn",
    )(x2d, *operands)


def _pad_cols(w):
    return jnp.concatenate([w, jnp.zeros((w.shape[0], PAD_COLS), w.dtype)], axis=1)


def _prep_layer(l, w_in, m_conv_w, m_conv_b, m_bias_i, m_bias_f, m_norm_g, lb_all, h_norm_g,
                r_conv_w, r_conv_b, r_w_rec, r_b_rec, r_w_in, r_b_in, r_lambda,
                w_branch_m, w_branch_h, w_branch_r, w_out, ln1_g, ln1_b, w_ff_gate, w_ff_up,
                w_ff_down, ln2_g, ln2_b):
    w = w_in[l]
    o = 0
    cols = {}
    for name, width in (("mq", WIDTH), ("mk", WIDTH), ("mv", WIDTH), ("mo", WIDTH),
                        ("mi", HEADS), ("mf", HEADS), ("hq", WIDTH), ("hf", WIDTH),
                        ("hi", WIDTH), ("hg", WIDTH), ("rx", WIDTH), ("rg", WIDTH),
                        ("gates", 3 * D_MODEL)):
        cols[name] = w[:, o:o + width]
        o += width
    w_main = jnp.concatenate(
        [cols[n] for n in ("mq", "mk", "rx", "mv", "mo", "hq", "hf", "hi", "hg", "rg", "gates",
                           "mi", "mf")] + [jnp.zeros((D_MODEL, LANES - 2 * HEADS), F32)],
        axis=1).astype(MXU_DTYPE)
    gate_bias = jnp.concatenate([m_bias_i[l], m_bias_f[l],
                                 jnp.zeros((LANES - 2 * HEADS,), F32)])[None, :]
    w_r = jnp.concatenate([jnp.concatenate([r_w_rec[l][n], r_w_in[l][n]], axis=1)
                           for n in range(R_BLOCKS)], axis=1)
    w_branch = jnp.concatenate([w_branch_m[l], w_branch_h[l], w_branch_r[l]], axis=1)
    return dict(
        w_main=w_main,
        conv_w=jnp.concatenate([m_conv_w[l], r_conv_w[l]], axis=1),
        conv_b=jnp.concatenate([m_conv_b[l], r_conv_b[l]])[None, :],
        gate_bias=gate_bias,
        m_norm_g=m_norm_g[l][None, :], h_norm_g=h_norm_g[l][None, :], lb=lb_all[l][None, :],
        w_r=_pad_cols(w_r.astype(MXU_DTYPE)),
        r_b=jnp.stack([r_b_rec[l], r_b_in[l]]), r_lambda=r_lambda[l][None, :],
        w_branch=_pad_cols(w_branch.astype(MXU_DTYPE)),
        w_out=_pad_cols(w_out[l].astype(MXU_DTYPE)),
        ln1_g=ln1_g[l][None, :], ln1_b=ln1_b[l][None, :],
        w_gu=jnp.concatenate([w_ff_gate[l], w_ff_up[l]], axis=1).astype(MXU_DTYPE),
        w_down=_pad_cols(w_ff_down[l].astype(MXU_DTYPE)),
        ln2_g=ln2_g[l][None, :], ln2_b=ln2_b[l][None, :],
    )


def kernel(x, w_in, m_conv_w, m_conv_b, m_bias_i, m_bias_f, m_norm_g, h_lower_bounds, h_norm_g, r_conv_w, r_conv_b, r_w_rec, r_b_rec, r_w_in, r_b_in, r_lambda, w_branch_m, w_branch_h, w_branch_r, w_out, ln1_g, ln1_b, w_ff_gate, w_ff_up, w_ff_down, ln2_g, ln2_b):
    depth = w_in.shape[0]
    alpha = float((2 * depth) ** 0.25)
    lb_all = jnp.cumsum(jax.nn.softmax(h_lower_bounds.astype(F32), axis=0), axis=0)
    lb_all = lb_all - lb_all[0]
    bsz, seq, d = x.shape
    for l in range(depth):
        p = _prep_layer(l, w_in, m_conv_w, m_conv_b, m_bias_i, m_bias_f, m_norm_g, lb_all,
                        h_norm_g, r_conv_w, r_conv_b, r_w_rec, r_b_rec, r_w_in, r_b_in, r_lambda,
                        w_branch_m, w_branch_h, w_branch_r, w_out, ln1_g, ln1_b, w_ff_gate,
                        w_ff_up, w_ff_down, ln2_g, ln2_b)
        x = _mixer_call(x, p, alpha)
        x = _ffn_call(x.reshape(bsz * seq, d), p, alpha).reshape(bsz, seq, d)
    return x
```
